```python
import jax, jax.numpy as jnp
from jax import lax
import numpy as np

D_MODEL = 2048
BATCH = 2
SEQ = 4096
DEPTH = 1

CHUNK = 64
Q_BLOCK = 128
D_MIX = D_MODEL
HGRN_WIDTH = D_MIX // 2
HGRN_DK = 128
HGRN_HEADS = HGRN_WIDTH // HGRN_DK
HGRN_DV = HGRN_WIDTH // HGRN_HEADS
FOX_WIDTH = D_MIX - HGRN_WIDTH
FOX_HEAD_DIM = 128
FOX_HEADS = FOX_WIDTH // FOX_HEAD_DIM
D_FF = ((8 * D_MODEL // 3 + 255) // 256) * 256
CONV_WIDTH = 3
EPS = 1e-6
FOX_GATE_BIAS_OFFSET = 2.0
IN_SIZES = (HGRN_HEADS * HGRN_DK, HGRN_HEADS * HGRN_DK, HGRN_WIDTH, HGRN_WIDTH,
            FOX_WIDTH, FOX_WIDTH, FOX_WIDTH, FOX_HEADS)
D_IN = sum(IN_SIZES)

kernel_name = "hymba_hgrn2_fox_convffn_block"


def rmsnorm(x, gain):
    x32 = x.astype(jnp.float32)
    y = x32 * lax.rsqrt(jnp.mean(x32 * x32, axis=-1, keepdims=True) + EPS) * gain.astype(jnp.float32)
    return y.astype(x.dtype)


def gla_chunkwise(q, k, v, logf):
    B, H, L, DK = q.shape
    DV = v.shape[-1]
    N = L // CHUNK
    r = lambda t: t.reshape(B, H, N, CHUNK, t.shape[-1])
    q, k, v, logf = r(q), r(k), r(v), r(logf)
    b = jnp.cumsum(logf, axis=-2)
    b_last = b[..., -1:, :]
    q_dec = q * jnp.exp(b)
    k_dec = k * jnp.exp(-b)
    k_end = k * jnp.exp(b_last - b)
    mask = jnp.tril(jnp.ones((CHUNK, CHUNK), dtype=bool))
    attn = jnp.einsum('bhnck,bhnsk->bhncs', q_dec, k_dec)
    attn = jnp.where(mask, attn, 0.0)
    o_intra = jnp.einsum('bhncs,bhnsv->bhncv', attn, v)
    chunk_decay = jnp.exp(b_last[..., 0, :])

    def step(S, xs):
        qd, ke, vv, dc = xs
        o = jnp.einsum('bhck,bhkv->bhcv', qd, S)
        S = dc[..., None] * S + jnp.einsum('bhck,bhcv->bhkv', ke, vv)
        return S, o

    xs = (jnp.moveaxis(q_dec, 2, 0), jnp.moveaxis(k_end, 2, 0),
          jnp.moveaxis(v, 2, 0), jnp.moveaxis(chunk_decay, 2, 0))
    S0 = jnp.zeros((B, H, DK, DV), jnp.float32)
    _, o_inter = lax.scan(step, S0, xs)
    o = o_intra + jnp.moveaxis(o_inter, 0, 2)
    return o.reshape(B, H, L, DV)


def hgrn2_mixer(q, f_logit, inp, g, lb, norm_gain):
    B, L, _ = q.shape
    dt = q.dtype
    heads = lambda t, d: t.reshape(B, L, HGRN_HEADS, d).transpose(0, 2, 1, 3).astype(jnp.float32)
    qh = heads(q, HGRN_DK) * (HGRN_DK ** -0.5)
    zh = heads(f_logit, HGRN_DK)
    vh = heads(inp, HGRN_DV)
    lbh = lb.astype(jnp.float32).reshape(HGRN_HEADS, HGRN_DK)[None, :, None, :]
    f = lbh + (1.0 - lbh) * jax.nn.sigmoid(zh)
    kh = 1.0 - f
    o = gla_chunkwise(qh, kh, vh, jnp.log(f))
    o = o * lax.rsqrt(jnp.mean(o * o, axis=-1, keepdims=True) + EPS) * norm_gain.astype(jnp.float32)
    o = o * jax.nn.silu(heads(g, HGRN_DV))
    return o.transpose(0, 2, 1, 3).reshape(B, L, HGRN_WIDTH).astype(dt)


def fox_mixer(q, k, v, f_logit, gate_bias):
    B, L, _ = q.shape
    dt = q.dtype
    heads = lambda t: t.reshape(B, L, FOX_HEADS, FOX_HEAD_DIM).transpose(0, 2, 1, 3)
    qh, kh, vh = heads(q), heads(k), heads(v)
    logf = jax.nn.log_sigmoid(f_logit.astype(jnp.float32) + gate_bias.astype(jnp.float32))
    c = jnp.cumsum(logf, axis=1).transpose(0, 2, 1)
    scale = FOX_HEAD_DIM ** -0.5
    kpos = jnp.arange(L)

    def one_block(blk):
        start = blk * Q_BLOCK
        qb = lax.dynamic_slice_in_dim(qh, start, Q_BLOCK, axis=2)
        cb = lax.dynamic_slice_in_dim(c, start, Q_BLOCK, axis=2)
        s = jnp.einsum('bhqd,bhkd->bhqk', qb, kh).astype(jnp.float32) * scale
        s = s + cb[..., :, None] - c[..., None, :]
        qpos = start + jnp.arange(Q_BLOCK)
        s = jnp.where(kpos[None, :] <= qpos[:, None], s, -jnp.inf)
        p = jax.nn.softmax(s, axis=-1)
        return jnp.einsum('bhqk,bhkd->bhqd', p.astype(vh.dtype), vh)

    out = lax.map(one_block, jnp.arange(L // Q_BLOCK))
    out = out.transpose(1, 0, 3, 2, 4).reshape(B, L, FOX_WIDTH)
    return out.astype(dt)


def conv_ffn(h, w_gate, w_up, conv_w, conv_b, w_down):
    L = h.shape[1]
    a = h @ w_gate
    a_pad = jnp.pad(a, ((0, 0), (CONV_WIDTH - 1, 0), (0, 0)))
    a = sum(conv_w[j] * a_pad[:, j:j + L] for j in range(CONV_WIDTH)) + conv_b
    return (jax.nn.silu(a) * (h @ w_up)) @ w_down


def setup_inputs(seed: int = 0) -> dict:
    key = jax.random.key(seed)
    ks = jax.random.split(key, 16)
    nrm = lambda k, shape, s: jax.random.normal(k, shape, jnp.float32) * s
    return {
        'x': nrm(ks[0], (BATCH, SEQ, D_MODEL), 1.0),
        'norm1_gain': 1.0 + nrm(ks[1], (DEPTH, D_MODEL), 0.02),
        'w_in': nrm(ks[2], (DEPTH, D_MODEL, D_IN), D_MODEL ** -0.5),
        'hgrn_lb_param': nrm(ks[3], (DEPTH + 1, HGRN_HEADS * HGRN_DK), 0.5),
        'hgrn_norm_gain': 1.0 + nrm(ks[4], (DEPTH, HGRN_DV), 0.02),
        'fox_gate_bias': FOX_GATE_BIAS_OFFSET + nrm(ks[5], (DEPTH, FOX_HEADS), 0.1),
        'w_out': nrm(ks[6], (DEPTH, D_MIX, D_MODEL), D_MIX ** -0.5),
        'norm2_gain': 1.0 + nrm(ks[7], (DEPTH, D_MODEL), 0.02),
        'w_ffn_gate': nrm(ks[8], (DEPTH, D_MODEL, D_FF), D_MODEL ** -0.5),
        'w_ffn_up': nrm(ks[9], (DEPTH, D_MODEL, D_FF), D_MODEL ** -0.5),
        'ffn_conv_w': nrm(ks[10], (DEPTH, CONV_WIDTH, D_FF), CONV_WIDTH ** -0.5),
        'ffn_conv_b': nrm(ks[11], (DEPTH, D_FF), 0.02),
        'w_ffn_down': nrm(ks[12], (DEPTH, D_FF, D_MODEL), D_FF ** -0.5),
        'final_norm_gain': 1.0 + nrm(ks[13], (D_MODEL,), 0.02),
    }


def reference(x, norm1_gain, w_in, hgrn_lb_param, hgrn_norm_gain, fox_gate_bias, w_out,
              norm2_gain, w_ffn_gate, w_ffn_up, ffn_conv_w, ffn_conv_b, w_ffn_down,
              final_norm_gain):
    split_points = [int(v) for v in np.cumsum(IN_SIZES)[:-1]]
    lb_all = jnp.cumsum(jax.nn.softmax(hgrn_lb_param.astype(jnp.float32), axis=0), axis=0)
    for l in range(DEPTH):
        h = rmsnorm(x, norm1_gain[l])
        proj = h @ w_in[l]
        q_a, f_a, i_a, g_a, q_b, k_b, v_b, f_b = jnp.split(proj, split_points, axis=-1)
        o_a = hgrn2_mixer(q_a, f_a, i_a, g_a, lb_all[l], hgrn_norm_gain[l])
        o_b = fox_mixer(q_b, k_b, v_b, f_b, fox_gate_bias[l])
        x = x + jnp.concatenate([o_a, o_b], axis=-1) @ w_out[l]
        h = rmsnorm(x, norm2_gain[l])
        x = x + conv_ffn(h, w_ffn_gate[l], w_ffn_up[l], ffn_conv_w[l], ffn_conv_b[l], w_ffn_down[l])
    return rmsnorm(x, final_norm_gain)
```

```python
import functools

import jax
import jax.numpy as jnp
from jax import lax
from jax.experimental import pallas as pl
from jax.experimental.pallas import tpu as pltpu

F32 = jnp.float32
BF16 = jnp.bfloat16

EPS = 1e-6
CHUNK = 64
HEAD = 128
CONV_WIDTH = 3
LANES = 128
SUBLANES = 8
NEG_BIG = -1e30


def _nt_dot(a, b):
    return lax.dot_general(a, b, (((1,), (1,)), ((), ())), preferred_element_type=F32)


def _tn_dot(a, b):
    return lax.dot_general(a, b, (((0,), (0,)), ((), ())), preferred_element_type=F32)


def _split3(x):
    hi = x.astype(BF16)
    r1 = x - hi.astype(F32)
    mid = r1.astype(BF16)
    lo = (r1 - mid.astype(F32)).astype(BF16)
    return hi, mid, lo


def _cumsum_rows(tri, x):
    hi, mid, lo = _split3(x)
    mm = lambda p: jnp.dot(tri, p, preferred_element_type=F32)
    return mm(hi) + mm(mid) + mm(lo)


def _tril_mask(n):
    row = lax.broadcasted_iota(jnp.int32, (n, n), 0)
    col = lax.broadcasted_iota(jnp.int32, (n, n), 1)
    return col <= row


def _in_proj_kernel(x_ref, g_ref, w_ref, wfb_ref, proj_ref, z_ref, fb_ref, h_scr, *, z_tile):
    j = pl.program_id(1)

    @pl.when(j == 0)
    def _():
        x = x_ref[...]
        ms = jnp.mean(x * x, axis=-1, keepdims=True)
        h = x * lax.rsqrt(ms + EPS) * g_ref[...]
        h_scr[...] = h.astype(BF16)
        fb_ref[...] = jnp.dot(h_scr[...], wfb_ref[...], preferred_element_type=F32)

    acc = jnp.dot(h_scr[...], w_ref[...], preferred_element_type=F32)
    proj_ref[...] = acc.astype(BF16)

    @pl.when(j == z_tile)
    def _():
        z_ref[...] = acc


def _in_proj(x2d, gain, w_main, w_fb, *, tm, tn, z_tile):
    m, d = x2d.shape
    n = w_main.shape[1]
    grid = (m // tm, n // tn)
    return pl.pallas_call(
        functools.partial(_in_proj_kernel, z_tile=z_tile),
        grid=grid,
        in_specs=[
            pl.BlockSpec((tm, d), lambda i, j: (i, 0)),
            pl.BlockSpec((1, d), lambda i, j: (0, 0)),
            pl.BlockSpec((d, tn), lambda i, j: (0, j)),
            pl.BlockSpec((d, LANES), lambda i, j: (0, 0)),
        ],
        out_specs=[
            pl.BlockSpec((tm, tn), lambda i, j: (i, j)),
            pl.BlockSpec((tm, tn), lambda i, j: (i, 0)),
            pl.BlockSpec((tm, LANES), lambda i, j: (i, 0)),
        ],
        out_shape=[
            jax.ShapeDtypeStruct((m, n), BF16),
            jax.ShapeDtypeStruct((m, tn), F32),
            jax.ShapeDtypeStruct((m, LANES), F32),
        ],
        scratch_shapes=[pltpu.VMEM((tm, d), BF16)],
        compiler_params=pltpu.CompilerParams(
            dimension_semantics=("arbitrary", "arbitrary"),
            vmem_limit_bytes=48 * 1024 * 1024,
        ),
        name="in_proj",
    )(x2d, gain, w_main, w_fb)


def _fox_gate_kernel(fb_ref, bias_ref, ct_ref, c_scr, *, blk):
    seq = fb_ref.shape[0]
    tri = _tril_mask(blk).astype(BF16)
    carry = jnp.zeros((1, LANES), F32)
    for r in range(seq // blk):
        xg = fb_ref[pl.ds(r * blk, blk), :] + bias_ref[...]
        logf = jnp.minimum(xg, 0.0) - jnp.log1p(jnp.exp(-jnp.abs(xg)))
        c = _cumsum_rows(tri, logf) + carry
        c_scr[pl.ds(r * blk, blk), :] = c
        carry = c[blk - 1:blk, :]
    ct = c_scr[...].T
    ct_ref[...] = ct[:ct_ref.shape[0], :]


def _fox_gate(fb, bias_row, *, batch, seq, heads):
    return pl.pallas_call(
        functools.partial(_fox_gate_kernel, blk=256),
        grid=(batch,),
        in_specs=[
            pl.BlockSpec((seq, LANES), lambda b: (b, 0)),
            pl.BlockSpec((1, LANES), lambda b: (0, 0)),
        ],
        out_specs=pl.BlockSpec((None, heads, seq), lambda b: (b, 0, 0)),
        out_shape=jax.ShapeDtypeStruct((batch, heads, seq), F32),
        scratch_shapes=[pltpu.VMEM((seq, LANES), F32)],
        compiler_params=pltpu.CompilerParams(dimension_semantics=("arbitrary",)),
        name="fox_gate",
    )(fb, bias_row)


def _hgrn_kernel(q_ref, z_ref, v_ref, g_ref, lbp_ref, gain_ref, o_ref, st_scr):
    @pl.when(pl.program_id(2) == 0)
    def _():
        st_scr[...] = jnp.zeros_like(st_scr)

    a = lbp_ref[:, 0, :]
    e = jnp.exp(a - jnp.max(a, axis=0, keepdims=True))
    lb = e[0:1, :] / jnp.sum(e, axis=0, keepdims=True)

    tri_mask = _tril_mask(CHUNK)
    tri = tri_mask.astype(BF16)
    scale = HEAD ** -0.5
    gain = gain_ref[...]
    st = st_scr[...]
    for c in range(q_ref.shape[0] // CHUNK):
        sl = pl.ds(c * CHUNK, CHUNK)
        z = z_ref[sl, :]
        f = lb + (1.0 - lb) * jax.nn.sigmoid(z)
        kk = 1.0 - f
        b = _cumsum_rows(tri, jnp.log(f))
        b_last = b[CHUNK - 1:CHUNK, :]
        q = q_ref[sl, :].astype(F32) * scale
        q_dec = (q * jnp.exp(b)).astype(BF16)
        k_dec = (kk * jnp.exp(-b)).astype(BF16)
        k_end = (kk * jnp.exp(b_last - b)).astype(BF16)
        v = v_ref[sl, :]
        attn = jnp.where(tri_mask, _nt_dot(q_dec, k_dec), 0.0).astype(BF16)
        o = jnp.dot(attn, v, preferred_element_type=F32) + _nt_dot(q_dec, st.astype(BF16))
        st = st * jnp.exp(b_last) + _tn_dot(v, k_end)
        o = o * lax.rsqrt(jnp.mean(o * o, axis=-1, keepdims=True) + EPS) * gain
        g = g_ref[sl, :].astype(F32)
        o_ref[sl, :] = (o * (g * jax.nn.sigmoid(g))).astype(o_ref.dtype)
    st_scr[...] = st


def _hgrn(proj, z, lbp, gain, *, batch, seq, heads, tl):
    m = proj.shape[0]
    nl = seq // tl
    row = lambda b, h, l: b * nl + l
    layers = lbp.shape[0]
    return pl.pallas_call(
        _hgrn_kernel,
        grid=(batch, heads, nl),
        in_specs=[
            pl.BlockSpec((tl, HEAD), lambda b, h, l: (row(b, h, l), h)),
            pl.BlockSpec((tl, HEAD), lambda b, h, l: (row(b, h, l), h)),
            pl.BlockSpec((tl, HEAD), lambda b, h, l: (row(b, h, l), 2 * heads + h)),
            pl.BlockSpec((tl, HEAD), lambda b, h, l: (row(b, h, l), 3 * heads + h)),
            pl.BlockSpec((layers, None, 1, HEAD), lambda b, h, l: (0, h, 0, 0)),
            pl.BlockSpec((1, HEAD), lambda b, h, l: (0, 0)),
        ],
        out_specs=pl.BlockSpec((tl, HEAD), lambda b, h, l: (row(b, h, l), h)),
        out_shape=jax.ShapeDtypeStruct((m, heads * HEAD), BF16),
        scratch_shapes=[pltpu.VMEM((HEAD, HEAD), F32)],
        compiler_params=pltpu.CompilerParams(
            dimension_semantics=("arbitrary", "arbitrary", "arbitrary")),
        name="hgrn",
    )(proj, z, proj, proj, lbp, gain)


def _fox_kernel(q_ref, k_ref, v_ref, ct_ref, o_ref, *, tq, tk):
    qi = pl.program_id(2)
    q = q_ref[...]
    scale = HEAD ** -0.5

    def tile(j, carry, masked):
        m, l, acc = carry
        start = pl.multiple_of(j * tk, tk)
        ks = k_ref[pl.ds(start, tk), :]
        vs = v_ref[pl.ds(start, tk), :]
        s = _nt_dot(q, ks) * scale - ct_ref[:, pl.ds(start, tk)]
        if masked:
            s = jnp.where(_tril_mask(tq), s, NEG_BIG)
        m_new = jnp.maximum(m, jnp.max(s, axis=-1, keepdims=True))
        alpha = jnp.exp(m - m_new)
        p = jnp.exp(s - m_new)
        l = alpha * l + jnp.sum(p, axis=-1, keepdims=True)
        acc = alpha * acc + jnp.dot(p.astype(BF16), vs, preferred_element_type=F32)
        return m_new, l, acc

    init = (jnp.full((tq, 1), NEG_BIG, F32), jnp.zeros((tq, 1), F32), jnp.zeros((tq, HEAD), F32))
    carry = lax.fori_loop(0, qi, lambda j, c: tile(j, c, False), init)
    _, l, acc = tile(qi, carry, True)
    o_ref[...] = (acc / l).astype(o_ref.dtype)


def _fox(proj3, ct, *, heads, tq):
    batch, seq, _ = proj3.shape
    nq = seq // tq
    col0 = 4 * heads
    return pl.pallas_call(
        functools.partial(_fox_kernel, tq=tq, tk=tq),
        grid=(batch, heads, nq),
        in_specs=[
            pl.BlockSpec((None, tq, HEAD), lambda b, h, i: (b, i, col0 + h)),
            pl.BlockSpec((None, seq, HEAD), lambda b, h, i: (b, 0, col0 + heads + h)),
            pl.BlockSpec((None, seq, HEAD), lambda b, h, i: (b, 0, col0 + 2 * heads + h)),
            pl.BlockSpec((None, 1, seq), lambda b, h, i: (b * heads + h, 0, 0)),
        ],
        out_specs=pl.BlockSpec((None, tq, HEAD), lambda b, h, i: (b, i, h)),
        out_shape=jax.ShapeDtypeStruct((batch, seq, heads * HEAD), BF16),
        compiler_params=pltpu.CompilerParams(
            dimension_semantics=("arbitrary", "arbitrary", "arbitrary")),
        name="fox",
    )(proj3, proj3, proj3, ct)


def _out_proj_kernel(x_ref, oa_ref, ob_ref, wa_ref, wb_ref, y_ref):
    y = jnp.dot(oa_ref[...], wa_ref[...], preferred_element_type=F32)
    y = y + jnp.dot(ob_ref[...], wb_ref[...], preferred_element_type=F32)
    y_ref[...] = x_ref[...] + y


def _out_proj(x2d, oa, ob, wa, wb, *, tm, tn):
    m, d = x2d.shape
    ka, kb = oa.shape[1], ob.shape[1]
    return pl.pallas_call(
        _out_proj_kernel,
        grid=(m // tm, d // tn),
        in_specs=[
            pl.BlockSpec((tm, tn), lambda i, j: (i, j)),
            pl.BlockSpec((tm, ka), lambda i, j: (i, 0)),
            pl.BlockSpec((tm, kb), lambda i, j: (i, 0)),
            pl.BlockSpec((ka, tn), lambda i, j: (0, j)),
            pl.BlockSpec((kb, tn), lambda i, j: (0, j)),
        ],
        out_specs=pl.BlockSpec((tm, tn), lambda i, j: (i, j)),
        out_shape=jax.ShapeDtypeStruct((m, d), F32),
        compiler_params=pltpu.CompilerParams(
            dimension_semantics=("arbitrary", "arbitrary"),
            vmem_limit_bytes=48 * 1024 * 1024,
        ),
        name="out_proj",
    )(x2d, oa, ob, wa, wb)


def _ffn_kernel(x_ref, g2_ref, wg_ref, wu_ref, cw_ref, cb_ref, wd_ref, gf_ref, y_ref,
                h_scr, acc_scr, tail_scr, *, tiles_per_seq):
    i = pl.program_id(0)
    f = pl.program_id(1)
    tm = x_ref.shape[0]

    @pl.when(f == 0)
    def _():
        x = x_ref[...]
        ms = jnp.mean(x * x, axis=-1, keepdims=True)
        h_scr[...] = (x * lax.rsqrt(ms + EPS) * g2_ref[...]).astype(BF16)
        acc_scr[...] = jnp.zeros_like(acc_scr)

    h = h_scr[...]
    a = jnp.dot(h, wg_ref[...], preferred_element_type=F32)
    u = jnp.dot(h, wu_ref[...], preferred_element_type=F32)

    tail = tail_scr[f]
    tail = jnp.where(i % tiles_per_seq == 0, 0.0, tail)
    t1 = tail[SUBLANES - 1:SUBLANES, :]
    t2 = tail[SUBLANES - 2:SUBLANES - 1, :]
    tail_scr[f] = a[tm - SUBLANES:, :]
    row = lax.broadcasted_iota(jnp.int32, a.shape, 0)
    a1 = jnp.where(row >= 1, pltpu.roll(a, 1, 0), t1)
    a2 = jnp.where(row >= 2, pltpu.roll(a, 2, 0), jnp.where(row == 1, t1, t2))
    cw = cw_ref[...]
    conv = cw[0:1, :] * a2 + cw[1:2, :] * a1 + cw[2:3, :] * a + cb_ref[...]
    act = (conv * jax.nn.sigmoid(conv) * u).astype(BF16)
    acc_scr[...] += jnp.dot(act, wd_ref[...], preferred_element_type=F32)

    @pl.when(f == pl.num_programs(1) - 1)
    def _():
        x2 = x_ref[...] + acc_scr[...]
        ms = jnp.mean(x2 * x2, axis=-1, keepdims=True)
        y_ref[...] = x2 * lax.rsqrt(ms + EPS) * gf_ref[...]


def _ffn(x1, g2, wg, wu, cw, cb, wd, gf, *, seq, tm, tf):
    m, d = x1.shape
    dff = wg.shape[1]
    nf = dff // tf
    return pl.pallas_call(
        functools.partial(_ffn_kernel, tiles_per_seq=seq // tm),
        grid=(m // tm, nf),
        in_specs=[
            pl.BlockSpec((tm, d), lambda i, f: (i, 0)),
            pl.BlockSpec((1, d), lambda i, f: (0, 0)),
            pl.BlockSpec((d, tf), lambda i, f: (0, f)),
            pl.BlockSpec((d, tf), lambda i, f: (0, f)),
            pl.BlockSpec((CONV_WIDTH, tf), lambda i, f: (0, f)),
            pl.BlockSpec((1, tf), lambda i, f: (0, f)),
            pl.BlockSpec((tf, d), lambda i, f: (f, 0)),
            pl.BlockSpec((1, d), lambda i, f: (0, 0)),
        ],
        out_specs=pl.BlockSpec((tm, d), lambda i, f: (i, 0)),
        out_shape=jax.ShapeDtypeStruct((m, d), F32),
        scratch_shapes=[
            pltpu.VMEM((tm, d), BF16),
            pltpu.VMEM((tm, d), F32),
            pltpu.VMEM((nf, SUBLANES, tf), F32),
        ],
        compiler_params=pltpu.CompilerParams(
            dimension_semantics=("arbitrary", "arbitrary"),
            vmem_limit_bytes=48 * 1024 * 1024,
        ),
        name="ffn",
    )(x1, g2, wg, wu, cw, cb, wd, gf)


def kernel(x, norm1_gain, w_in, hgrn_lb_param, hgrn_norm_gain, fox_gate_bias, w_out, norm2_gain,
           w_ffn_gate, w_ffn_up, ffn_conv_w, ffn_conv_b, w_ffn_down, final_norm_gain):
    batch, seq, d = x.shape
    assert w_in.shape[0] == 1, "one trunk layer"
    heads = (w_out.shape[1] // 2) // HEAD
    group = heads * HEAD
    n_main = 7 * group
    m = batch * seq
    x2d = x.reshape(m, d)

    w_in0 = w_in[0]
    w_main = w_in0[:, :n_main].astype(BF16)
    w_fb = jnp.pad(w_in0[:, n_main:], ((0, 0), (0, LANES - heads))).astype(BF16)
    proj, z, fb = _in_proj(x2d, norm1_gain[0][None, :], w_main, w_fb, tm=512, tn=group, z_tile=1)

    bias_row = jnp.pad(fox_gate_bias[0], (0, LANES - heads))[None, :]
    ct = _fox_gate(fb, bias_row, batch=batch, seq=seq, heads=heads)
    ct = ct.reshape(batch * heads, 1, seq)

    lbp = hgrn_lb_param.reshape(hgrn_lb_param.shape[0], heads, 1, HEAD)
    o_a = _hgrn(proj, z, lbp, hgrn_norm_gain[0][None, :], batch=batch, seq=seq, heads=heads, tl=512)

    o_b = _fox(proj.reshape(batch, seq, n_main), ct, heads=heads, tq=512).reshape(m, group)

    w_out0 = w_out[0].astype(BF16)
    x1 = _out_proj(x2d, o_a, o_b, w_out0[:group], w_out0[group:], tm=1024, tn=1024)

    y = _ffn(x1, norm2_gain[0][None, :], w_ffn_gate[0].astype(BF16), w_ffn_up[0].astype(BF16),
             ffn_conv_w[0], ffn_conv_b[0][None, :], w_ffn_down[0].astype(BF16),
             final_norm_gain[None, :], seq=seq, tm=512, tf=512)
    return y.reshape(batch, seq, d)
```
